```python
import math
import jax, jax.numpy as jnp
from jax import lax
import numpy as np

D_MODEL = 2048
BATCH = 2
SEQ = 4096
DEPTH = 1

GLA_WIDTH = D_MODEL // 2
FOX_WIDTH = D_MODEL - GLA_WIDTH
GLA_HEADS = 4
GLA_DV = GLA_WIDTH // GLA_HEADS
GLA_DK = GLA_DV // 2
GLA_LOWRANK = 16
GLA_TAU = 16.0
GLA_CHUNK = 64
FOX_HEADS = 8
FOX_DH = FOX_WIDTH // FOX_HEADS
Q_BLOCK = 128
FOX_GATE_BIAS = 2.0
SPLITS = (GLA_HEADS * GLA_DK, GLA_HEADS * GLA_DK, GLA_WIDTH, GLA_WIDTH, GLA_LOWRANK,
          FOX_WIDTH, FOX_WIDTH, FOX_WIDTH, FOX_HEADS)
D_IN = sum(SPLITS)
MEM_LEN = 256
XATTN_HEADS = 4
XATTN_DH = D_MODEL // XATTN_HEADS
N_EXPERTS = 32
TOP_K = 4
D_FF = D_MODEL
SWIGLU_LIMIT = 7.0
SWIGLU_ALPHA = 1.702
EXPERT_BLOCK = 128
EPS = 1e-6

kernel_name = "hymba_gla_fox_memxattn_gptoss_moe"


def rmsnorm(x, g):
    xf = x.astype(jnp.float32)
    y = xf * lax.rsqrt(jnp.mean(xf * xf, axis=-1, keepdims=True) + EPS)
    return (y * g.astype(jnp.float32)).astype(x.dtype)


def gla_chunked(q, k, v, log_a):
    B, S, H, DK = q.shape
    DV = v.shape[-1]
    C = GLA_CHUNK
    N = S // C

    def chunks(t):
        return t.astype(jnp.float32).reshape(B, N, C, H, t.shape[-1]).transpose(0, 3, 1, 2, 4)

    q, k, v, g = chunks(q), chunks(k), chunks(v), chunks(log_a)
    G = jnp.cumsum(g, axis=3)
    G_last = G[:, :, :, -1:, :]
    q_in = q * jnp.exp(G)
    k_in = k * jnp.exp(-G)
    causal = jnp.tril(jnp.ones((C, C), dtype=bool))
    A = jnp.where(causal, jnp.einsum('bhnck,bhnsk->bhncs', q_in, k_in), 0.0)
    o_intra = jnp.einsum('bhncs,bhnsv->bhncv', A, v)
    k_to_end = k * jnp.exp(G_last - G)
    chunk_kv = jnp.einsum('bhnck,bhncv->bhnkv', k_to_end, v)
    chunk_decay = jnp.exp(G_last[:, :, :, 0, :])

    def step(state, inp):
        decay_n, kv_n = inp
        return decay_n[..., None] * state + kv_n, state

    _, states_before = lax.scan(step, jnp.zeros((B, H, DK, DV), jnp.float32),
                                (jnp.moveaxis(chunk_decay, 2, 0), jnp.moveaxis(chunk_kv, 2, 0)))
    states_before = jnp.moveaxis(states_before, 0, 2)
    o_inter = jnp.einsum('bhnck,bhnkv->bhncv', q_in, states_before)
    return (o_intra + o_inter).transpose(0, 2, 3, 1, 4).reshape(B, S, H, DV)


def forgetting_attention(q, k, v, log_f):
    B, S, H, d = q.shape
    nb = S // Q_BLOCK
    scale = d ** -0.5
    k = k.transpose(0, 2, 1, 3)
    v = v.transpose(0, 2, 1, 3)
    c = jnp.cumsum(log_f, axis=1).transpose(0, 2, 1)
    qb = q.transpose(0, 2, 1, 3).reshape(B, H, nb, Q_BLOCK, d).transpose(2, 0, 1, 3, 4)
    cb = c.reshape(B, H, nb, Q_BLOCK).transpose(2, 0, 1, 3)
    kpos = jnp.arange(S)

    def block(args):
        q_blk, c_blk, i = args
        qpos = i * Q_BLOCK + jnp.arange(Q_BLOCK)
        logits = (jnp.einsum('bhqd,bhkd->bhqk', q_blk, k).astype(jnp.float32) * scale
                  + c_blk[..., :, None] - c[..., None, :])
        logits = jnp.where(kpos[None, :] <= qpos[:, None], logits, -jnp.inf)
        p = jax.nn.softmax(logits, axis=-1)
        return jnp.einsum('bhqk,bhkd->bhqd', p.astype(v.dtype), v)

    out = lax.map(block, (qb, cb, jnp.arange(nb)))
    return out.transpose(1, 0, 3, 2, 4).reshape(B, S, H * d)


def hybrid_mixer(xn, w_in, w_alpha_up, b_alpha, b_fgate, gla_norm, w_out):
    B, S, _ = xn.shape
    proj = xn @ w_in
    idx = [int(i) for i in np.cumsum(SPLITS)[:-1]]
    gq, gk, gv, gr, ga, fq, fk, fv, ff = jnp.split(proj, idx, axis=-1)
    q = gq.reshape(B, S, GLA_HEADS, GLA_DK) * (GLA_DK ** -0.5)
    k = gk.reshape(B, S, GLA_HEADS, GLA_DK)
    v = gv.reshape(B, S, GLA_HEADS, GLA_DV)
    log_a = jax.nn.log_sigmoid((ga @ w_alpha_up + b_alpha).astype(jnp.float32)) / GLA_TAU
    log_a = log_a.reshape(B, S, GLA_HEADS, GLA_DK)
    o_gla = rmsnorm(gla_chunked(q, k, v, log_a), gla_norm).astype(xn.dtype)
    o_gla = o_gla.reshape(B, S, GLA_WIDTH) * jax.nn.silu(gr)
    log_f = jax.nn.log_sigmoid((ff + b_fgate).astype(jnp.float32))
    o_fox = forgetting_attention(fq.reshape(B, S, FOX_HEADS, FOX_DH),
                                 fk.reshape(B, S, FOX_HEADS, FOX_DH),
                                 fv.reshape(B, S, FOX_HEADS, FOX_DH), log_f)
    return jnp.concatenate([o_gla, o_fox.astype(xn.dtype)], axis=-1) @ w_out


def memory_cross_attention(xn, mem_n, w_xq, w_xkv, w_xo):
    B, S, _ = xn.shape
    M = mem_n.shape[1]
    q = (xn @ w_xq).reshape(B, S, XATTN_HEADS, XATTN_DH)
    k, v = jnp.split(mem_n @ w_xkv, 2, axis=-1)
    k = k.reshape(B, M, XATTN_HEADS, XATTN_DH)
    v = v.reshape(B, M, XATTN_HEADS, XATTN_DH)
    logits = jnp.einsum('bshd,bmhd->bhsm', q, k).astype(jnp.float32) * (XATTN_DH ** -0.5)
    p = jax.nn.softmax(logits, axis=-1).astype(v.dtype)
    o = jnp.einsum('bhsm,bmhd->bshd', p, v).reshape(B, S, XATTN_HEADS * XATTN_DH)
    return o @ w_xo


def routed_experts(xn, w_router, b_router, w_gate_up, b_gate_up, w_down, b_down):
    B, S, D = xn.shape
    T = B * S
    TK = T * TOP_K
    xf = xn.reshape(T, D)
    logits = (xf @ w_router + b_router).astype(jnp.float32)
    top_vals, top_idx = lax.top_k(logits, TOP_K)
    gates = jax.nn.softmax(top_vals, axis=-1).astype(xn.dtype)
    flat_e = top_idx.reshape(-1).astype(jnp.int32)
    flat_tok = jnp.arange(TK, dtype=jnp.int32) // TOP_K
    flat_gate = gates.reshape(-1)
    order = jnp.argsort(flat_e, stable=True)
    sorted_e = flat_e[order]
    counts = jnp.bincount(flat_e, length=N_EXPERTS)
    padded = ((counts + EXPERT_BLOCK - 1) // EXPERT_BLOCK) * EXPERT_BLOCK
    pad_end = jnp.cumsum(padded)
    pad_start = pad_end - padded
    start = jnp.cumsum(counts) - counts
    dest = pad_start[sorted_e] + jnp.arange(TK, dtype=jnp.int32) - start[sorted_e]
    NB = -(-TK // EXPERT_BLOCK) + N_EXPERTS
    P = NB * EXPERT_BLOCK
    slot_tok = jnp.zeros((P,), jnp.int32).at[dest].set(flat_tok[order])
    slot_gate = jnp.zeros((P,), xn.dtype).at[dest].set(flat_gate[order])
    block_start = jnp.arange(NB, dtype=jnp.int32) * EXPERT_BLOCK
    block_e = jnp.minimum(jnp.sum(pad_end[None, :] <= block_start[:, None], axis=1),
                          N_EXPERTS - 1).astype(jnp.int32)
    xs = xf[slot_tok].reshape(NB, EXPERT_BLOCK, D)

    def expert_block(args):
        xb, e = args
        gu = xb @ w_gate_up[e] + b_gate_up[e]
        g, u = gu[:, 0::2], gu[:, 1::2]
        g = jnp.minimum(g, SWIGLU_LIMIT)
        u = jnp.clip(u, -SWIGLU_LIMIT, SWIGLU_LIMIT)
        h = (u + 1.0) * (g * jax.nn.sigmoid(SWIGLU_ALPHA * g))
        return h @ w_down[e] + b_down[e]

    y = lax.map(expert_block, (xs, block_e)).reshape(P, D)
    out = jnp.zeros((T, D), xn.dtype).at[slot_tok].add(y * slot_gate[:, None])
    return out.reshape(B, S, D)


def setup_inputs(seed: int = 0) -> dict:
    key = jax.random.key(seed)
    ks = jax.random.split(key, 24)
    L, D, E, F = DEPTH, D_MODEL, N_EXPERTS, D_FF
    XW = XATTN_HEADS * XATTN_DH

    def nrm(k, shape, scale):
        return jax.random.normal(k, shape, jnp.float32) * scale

    return {
        "x": nrm(ks[0], (BATCH, SEQ, D), 1.0),
        "mem": nrm(ks[1], (BATCH, MEM_LEN, D), 1.0),
        "norm_mix": 1.0 + nrm(ks[2], (L, D), 0.02),
        "w_in": nrm(ks[3], (L, D, D_IN), D ** -0.5),
        "w_alpha_up": nrm(ks[4], (L, GLA_LOWRANK, GLA_HEADS * GLA_DK), GLA_LOWRANK ** -0.5),
        "b_alpha": nrm(ks[5], (L, GLA_HEADS * GLA_DK), 0.1),
        "b_fgate": FOX_GATE_BIAS + nrm(ks[6], (L, FOX_HEADS), 0.1),
        "gla_norm": 1.0 + nrm(ks[7], (L, GLA_DV), 0.02),
        "w_out": nrm(ks[8], (L, D, D), D ** -0.5),
        "norm_xattn": 1.0 + nrm(ks[9], (L, D), 0.02),
        "norm_mem": 1.0 + nrm(ks[10], (L, D), 0.02),
        "w_xq": nrm(ks[11], (L, D, XW), D ** -0.5),
        "w_xkv": nrm(ks[12], (L, D, 2 * XW), D ** -0.5),
        "w_xo": nrm(ks[13], (L, XW, D), XW ** -0.5),
        "norm_moe": 1.0 + nrm(ks[14], (L, D), 0.02),
        "w_router": nrm(ks[15], (L, D, E), D ** -0.5),
        "b_router": nrm(ks[16], (L, E), 0.01),
        "w_gate_up": nrm(ks[17], (L, E, D, 2 * F), D ** -0.5),
        "b_gate_up": nrm(ks[18], (L, E, 2 * F), 0.01),
        "w_down": nrm(ks[19], (L, E, F, D), F ** -0.5),
        "b_down": nrm(ks[20], (L, E, D), 0.01),
        "norm_final": 1.0 + nrm(ks[21], (D,), 0.02),
    }


def reference(x, mem, norm_mix, w_in, w_alpha_up, b_alpha, b_fgate, gla_norm, w_out,
              norm_xattn, norm_mem, w_xq, w_xkv, w_xo, norm_moe, w_router, b_router,
              w_gate_up, b_gate_up, w_down, b_down, norm_final):
    h = x
    for l in range(DEPTH):
        h = h + hybrid_mixer(rmsnorm(h, norm_mix[l]), w_in[l], w_alpha_up[l], b_alpha[l],
                             b_fgate[l], gla_norm[l], w_out[l])
        h = h + memory_cross_attention(rmsnorm(h, norm_xattn[l]), rmsnorm(mem, norm_mem[l]),
                                       w_xq[l], w_xkv[l], w_xo[l])
        h = h + routed_experts(rmsnorm(h, norm_moe[l]), w_router[l], b_router[l],
                               w_gate_up[l], b_gate_up[l], w_down[l], b_down[l])
    return rmsnorm(h, norm_final)
```

```python
import functools

import numpy as np
import jax
import jax.numpy as jnp
from jax import lax
from jax.experimental import pallas as pl
from jax.experimental.pallas import tpu as pltpu

D_MODEL = 2048
BATCH = 2
SEQ = 4096
T_TOK = BATCH * SEQ
GLA_HEADS = 4
GLA_DK = 128
GLA_DV = 256
GLA_LOWRANK = 16
GLA_TAU = 16.0
GLA_CHUNK = 64
FOX_HEADS = 8
FOX_DH = 128
GLA_COLS = 2 * GLA_HEADS * GLA_DK + 2 * GLA_HEADS * GLA_DV
FOX_COLS = 3 * FOX_HEADS * FOX_DH
FOX_COL0 = GLA_COLS + GLA_LOWRANK
FF_COL0 = FOX_COL0 + FOX_COLS
MEM_LEN = 256
XATTN_HEADS = 4
XATTN_DH = 512
N_EXPERTS = 32
TOP_K = 4
D_FF = 2048
SWIGLU_LIMIT = 7.0
SWIGLU_ALPHA = 1.702
EPS = 1e-6

V7X_LANES = 128
V7X_VMEM_LIMIT_BYTES = 56 * 1024 * 1024

F32 = jnp.float32
BF16 = jnp.bfloat16

_NT = (((1,), (1,)), ((), ()))
_TN = (((0,), (0,)), ((), ()))


def _params(semantics, vmem_bytes=V7X_VMEM_LIMIT_BYTES):
    return pltpu.CompilerParams(dimension_semantics=semantics, vmem_limit_bytes=vmem_bytes)


def _rmsnorm(x, g):
    return x * lax.rsqrt(jnp.mean(x * x, axis=-1, keepdims=True) + EPS) * g


def _log_sigmoid(x):
    return jnp.minimum(x, 0.0) - jnp.log1p(jnp.exp(-jnp.abs(x)))


def _split3(x):
    a = x.astype(BF16)
    r = x - a.astype(F32)
    b = r.astype(BF16)
    c = (r - b.astype(F32)).astype(BF16)
    return a, b, c


def _dot(a, b):
    return jnp.dot(a, b, preferred_element_type=F32)


def _dotg(a, b, dims):
    return lax.dot_general(a, b, dims, preferred_element_type=F32)


INPROJ_TM = 512
INPROJ_TN = 1024
INPROJ_NJ_GLA = GLA_COLS // INPROJ_TN


def _inproj_kernel(x_ref, g_ref, w_ref, ws_ref, og_ref, of_ref, os_ref, xn_ref):
    j = pl.program_id(1)

    @pl.when(j == 0)
    def _():
        xn_ref[...] = _rmsnorm(x_ref[...], g_ref[...]).astype(BF16)
        os_ref[...] = _dot(xn_ref[...], ws_ref[...])

    acc = _dot(xn_ref[...], w_ref[...])

    @pl.when(j < INPROJ_NJ_GLA)
    def _():
        og_ref[...] = acc

    @pl.when(j >= INPROJ_NJ_GLA)
    def _():
        of_ref[...] = acc.astype(BF16)


def _inproj(x, gain, w_main, w_small):
    tm, tn = INPROJ_TM, INPROJ_TN
    nj = (GLA_COLS + FOX_COLS) // tn
    ng = INPROJ_NJ_GLA
    return pl.pallas_call(
        _inproj_kernel,
        grid=(T_TOK // tm, nj),
        in_specs=[
            pl.BlockSpec((tm, D_MODEL), lambda i, j: (i, 0)),
            pl.BlockSpec((1, D_MODEL), lambda i, j: (0, 0)),
            pl.BlockSpec((D_MODEL, tn), lambda i, j: (0, j)),
            pl.BlockSpec((D_MODEL, V7X_LANES), lambda i, j: (0, 0)),
        ],
        out_specs=[
            pl.BlockSpec((tm, tn), lambda i, j: (i, jnp.minimum(j, ng - 1))),
            pl.BlockSpec((tm, tn), lambda i, j: (i, jnp.maximum(j - ng, 0))),
            pl.BlockSpec((tm, V7X_LANES), lambda i, j: (i, 0)),
        ],
        out_shape=[
            jax.ShapeDtypeStruct((T_TOK, GLA_COLS), F32),
            jax.ShapeDtypeStruct((T_TOK, FOX_COLS), BF16),
            jax.ShapeDtypeStruct((T_TOK, V7X_LANES), F32),
        ],
        scratch_shapes=[pltpu.VMEM((tm, D_MODEL), BF16)],
        compiler_params=_params(("parallel", "arbitrary")),
        name="inproj",
    )(x, gain, w_main, w_small)


def _norm_mm_kernel(x_ref, g_ref, w_ref, o_ref, xn_ref):
    @pl.when(pl.program_id(1) == 0)
    def _():
        xn_ref[...] = _rmsnorm(x_ref[...], g_ref[...]).astype(BF16)

    o_ref[...] = _dot(xn_ref[...], w_ref[...]).astype(o_ref.dtype)


def _norm_mm(x, gain, w, tm, tn, name):
    m, k = x.shape
    n = w.shape[1]
    return pl.pallas_call(
        _norm_mm_kernel,
        grid=(m // tm, n // tn),
        in_specs=[
            pl.BlockSpec((tm, k), lambda i, j: (i, 0)),
            pl.BlockSpec((1, k), lambda i, j: (0, 0)),
            pl.BlockSpec((k, tn), lambda i, j: (0, j)),
        ],
        out_specs=pl.BlockSpec((tm, tn), lambda i, j: (i, j)),
        out_shape=jax.ShapeDtypeStruct((m, n), BF16),
        scratch_shapes=[pltpu.VMEM((tm, k), BF16)],
        compiler_params=_params(("parallel", "arbitrary")),
        name=name,
    )(x, gain, w)


FCUM_TB = 512


def _fcum_kernel(s_ref, bf_ref, ccol_ref, crow_ref, carry_c, carry_r):
    @pl.when(pl.program_id(1) == 0)
    def _():
        carry_c[...] = jnp.zeros_like(carry_c)
        carry_r[...] = jnp.zeros_like(carry_r)

    tb = FCUM_TB
    lo, hi = GLA_LOWRANK, GLA_LOWRANK + FOX_HEADS
    lf = _log_sigmoid(s_ref[...] + bf_ref[...])
    r = lax.broadcasted_iota(jnp.int32, (tb, tb), 0)
    c = lax.broadcasted_iota(jnp.int32, (tb, tb), 1)
    lower = jnp.where(r >= c, 1.0, 0.0).astype(BF16)
    upper = jnp.where(r <= c, 1.0, 0.0).astype(BF16)
    ccol = jnp.broadcast_to(carry_c[...], (tb, V7X_LANES))
    crow = jnp.broadcast_to(carry_r[...], (V7X_LANES, tb))
    for p in _split3(lf):
        ccol = ccol + _dot(lower, p)
        crow = crow + _dotg(p, upper, _TN)
    ccol_ref[0] = ccol[:, lo:hi]
    crow_ref[0] = crow[lo:hi, :]
    carry_c[...] = ccol[tb - 1:tb, :]
    carry_r[...] = crow[:, tb - 1:tb]


def _fcum(small, b_fgate):
    tb = FCUM_TB
    nb = SEQ // tb
    return pl.pallas_call(
        _fcum_kernel,
        grid=(BATCH, nb),
        in_specs=[
            pl.BlockSpec((tb, V7X_LANES), lambda b, s: (b * nb + s, 0)),
            pl.BlockSpec((1, V7X_LANES), lambda b, s: (0, 0)),
        ],
        out_specs=[
            pl.BlockSpec((1, tb, FOX_HEADS), lambda b, s: (b, s, 0)),
            pl.BlockSpec((1, FOX_HEADS, tb), lambda b, s: (b, 0, s)),
        ],
        out_shape=[
            jax.ShapeDtypeStruct((BATCH, SEQ, FOX_HEADS), F32),
            jax.ShapeDtypeStruct((BATCH, FOX_HEADS, SEQ), F32),
        ],
        scratch_shapes=[pltpu.VMEM((1, V7X_LANES), F32), pltpu.VMEM((V7X_LANES, 1), F32)],
        compiler_params=_params(("parallel", "arbitrary")),
        name="fox_gate_cumsum",
    )(small, b_fgate)


GLA_TS = 512


def _gla_kernel(q_ref, k_ref, v_ref, r_ref, ga_ref, wa_ref, ba_ref, gn_ref, o_ref, st_ref):
    @pl.when(pl.program_id(2) == 0)
    def _():
        st_ref[...] = jnp.zeros_like(st_ref)

    cs = GLA_CHUNK
    row = lax.broadcasted_iota(jnp.int32, (cs, cs), 0)
    col = lax.broadcasted_iota(jnp.int32, (cs, cs), 1)
    causal = row >= col
    tri = jnp.where(causal, 1.0, 0.0).astype(BF16)
    ones = jnp.ones((cs, GLA_DK), BF16)
    wa = wa_ref[...].astype(BF16)
    ba = ba_ref[...]
    gn = gn_ref[...]
    qscale = GLA_DK ** -0.5
    state = st_ref[...]
    for ci in range(GLA_TS // cs):
        sl = pl.ds(ci * cs, cs)
        ga = ga_ref[sl, :].astype(BF16)
        log_a = _log_sigmoid(_dot(ga, wa) + ba) * (1.0 / GLA_TAU)
        g_cum = jnp.zeros((cs, GLA_DK), F32)
        g_end = jnp.zeros((GLA_DK, GLA_DK), F32)
        for p in _split3(log_a):
            g_cum = g_cum + _dot(tri, p)
            g_end = g_end + _dotg(p, ones, _TN)
        g_last = g_cum[cs - 1:cs, :]
        q = q_ref[sl, :] * qscale
        k = k_ref[sl, :]
        v = v_ref[sl, :].astype(BF16)
        q_in = (q * jnp.exp(g_cum)).astype(BF16)
        k_in = (k * jnp.exp(-g_cum)).astype(BF16)
        a = jnp.where(causal, _dotg(q_in, k_in, _NT), 0.0).astype(BF16)
        o = _dot(a, v) + _dot(q_in, state.astype(BF16))
        k_end = (k * jnp.exp(g_last - g_cum)).astype(BF16)
        kv = _dotg(k_end, v, _TN)
        decay = jnp.exp(g_end)
        state = jnp.concatenate([decay, decay], axis=1) * state + kv
        y = _rmsnorm(o, gn)
        r = r_ref[sl, :]
        o_ref[sl, :] = (y * (r * jax.nn.sigmoid(r))).astype(BF16)
    st_ref[...] = state


def _gla(proj_gla, small, w_alpha_up, b_alpha, gla_norm):
    ts = GLA_TS
    ns = SEQ // ts
    kq = GLA_HEADS
    v0 = 2 * GLA_HEADS * GLA_DK // GLA_DV
    r0 = v0 + GLA_HEADS
    return pl.pallas_call(
        _gla_kernel,
        grid=(BATCH, GLA_HEADS, ns),
        in_specs=[
            pl.BlockSpec((ts, GLA_DK), lambda b, h, s: (b * ns + s, h)),
            pl.BlockSpec((ts, GLA_DK), lambda b, h, s: (b * ns + s, kq + h)),
            pl.BlockSpec((ts, GLA_DV), lambda b, h, s: (b * ns + s, v0 + h)),
            pl.BlockSpec((ts, GLA_DV), lambda b, h, s: (b * ns + s, r0 + h)),
            pl.BlockSpec((ts, V7X_LANES), lambda b, h, s: (b * ns + s, 0)),
            pl.BlockSpec((V7X_LANES, GLA_DK), lambda b, h, s: (0, h)),
            pl.BlockSpec((1, GLA_DK), lambda b, h, s: (0, h)),
            pl.BlockSpec((1, GLA_DV), lambda b, h, s: (0, 0)),
        ],
        out_specs=pl.BlockSpec((ts, GLA_DV), lambda b, h, s: (b * ns + s, h)),
        out_shape=jax.ShapeDtypeStruct((T_TOK, GLA_HEADS * GLA_DV), BF16),
        scratch_shapes=[pltpu.VMEM((GLA_DK, GLA_DV), F32)],
        compiler_params=_params(("parallel", "parallel", "arbitrary")),
        name="gla",
    )(proj_gla, proj_gla, proj_gla, proj_gla, small, w_alpha_up, b_alpha, gla_norm)


FOX_TQ = 512
FOX_TK = 512


def _fox_kernel(q_ref, k_ref, v_ref, ccol_ref, crow_ref, o_ref, m_sc, l_sc, acc_sc, cq_sc):
    h = pl.program_id(1)
    qi = pl.program_id(2)
    kj = pl.program_id(3)

    @pl.when(kj == 0)
    def _():
        m_sc[...] = jnp.full_like(m_sc, -jnp.inf)
        l_sc[...] = jnp.zeros_like(l_sc)
        acc_sc[...] = jnp.zeros_like(acc_sc)
        lane = lax.broadcasted_iota(jnp.int32, (FOX_TQ, FOX_HEADS), 1)
        cq_sc[...] = jnp.sum(jnp.where(lane == h, ccol_ref[0], 0.0), axis=1, keepdims=True)

    def step(diagonal):
        s = _dotg(q_ref[...], k_ref[...], _NT) * (FOX_DH ** -0.5)
        s = s + cq_sc[...] - crow_ref[0, pl.ds(h, 1), :]
        if diagonal:
            row = lax.broadcasted_iota(jnp.int32, (FOX_TQ, FOX_TK), 0)
            col = lax.broadcasted_iota(jnp.int32, (FOX_TQ, FOX_TK), 1)
            s = jnp.where(col <= row, s, -jnp.inf)
        m_prev = m_sc[...]
        m_new = jnp.maximum(m_prev, jnp.max(s, axis=1, keepdims=True))
        alpha = jnp.exp(m_prev - m_new)
        p = jnp.exp(s - m_new)
        l_sc[...] = alpha * l_sc[...] + jnp.sum(p, axis=1, keepdims=True)
        acc_sc[...] = alpha * acc_sc[...] + _dot(p.astype(BF16), v_ref[...])
        m_sc[...] = m_new

    @pl.when(kj < qi)
    def _():
        step(False)

    @pl.when(kj == qi)
    def _():
        step(True)
        o_ref[...] = (acc_sc[...] / l_sc[...]).astype(BF16)


def _fox(proj_fox, ccol, crow):
    tq, tk = FOX_TQ, FOX_TK
    nq, nk = SEQ // tq, SEQ // tk
    return pl.pallas_call(
        _fox_kernel,
        grid=(BATCH, FOX_HEADS, nq, nk),
        in_specs=[
            pl.BlockSpec((tq, FOX_DH), lambda b, h, i, j: (b * nq + i, h)),
            pl.BlockSpec((tk, FOX_DH), lambda b, h, i, j: (b * nk + jnp.minimum(j, i), FOX_HEADS + h)),
            pl.BlockSpec((tk, FOX_DH), lambda b, h, i, j: (b * nk + jnp.minimum(j, i), 2 * FOX_HEADS + h)),
            pl.BlockSpec((1, tq, FOX_HEADS), lambda b, h, i, j: (b, i, 0)),
            pl.BlockSpec((1, FOX_HEADS, tk), lambda b, h, i, j: (b, 0, jnp.minimum(j, i))),
        ],
        out_specs=pl.BlockSpec((tq, FOX_DH), lambda b, h, i, j: (b * nq + i, h)),
        out_shape=jax.ShapeDtypeStruct((T_TOK, FOX_HEADS * FOX_DH), BF16),
        scratch_shapes=[
            pltpu.VMEM((tq, 1), F32),
            pltpu.VMEM((tq, 1), F32),
            pltpu.VMEM((tq, FOX_DH), F32),
            pltpu.VMEM((tq, 1), F32),
        ],
        compiler_params=_params(("parallel", "parallel", "parallel", "arbitrary")),
        name="fox_attention",
    )(proj_fox, proj_fox, proj_fox, ccol, crow)


OUTPROJ_TM = 1024
OUTPROJ_TN = 1024


def _outproj_kernel(a_ref, b_ref, wa_ref, wb_ref, r_ref, o_ref):
    o_ref[...] = r_ref[...] + _dot(a_ref[...], wa_ref[...]) + _dot(b_ref[...], wb_ref[...])


def _outproj(o_gla, o_fox, w_out, resid):
    tm, tn = OUTPROJ_TM, OUTPROJ_TN
    half = D_MODEL // 2
    return pl.pallas_call(
        _outproj_kernel,
        grid=(T_TOK // tm, D_MODEL // tn),
        in_specs=[
            pl.BlockSpec((tm, half), lambda i, j: (i, 0)),
            pl.BlockSpec((tm, half), lambda i, j: (i, 0)),
            pl.BlockSpec((half, tn), lambda i, j: (0, j)),
            pl.BlockSpec((half, tn), lambda i, j: (1, j)),
            pl.BlockSpec((tm, tn), lambda i, j: (i, j)),
        ],
        out_specs=pl.BlockSpec((tm, tn), lambda i, j: (i, j)),
        out_shape=jax.ShapeDtypeStruct((T_TOK, D_MODEL), F32),
        compiler_params=_params(("parallel", "arbitrary")),
        name="mixer_outproj",
    )(o_gla, o_fox, w_out, w_out, resid)


XATTN_TM = 512
XATTN_TN = 1024


def _xattn_kernel(q_ref, k_ref, v_ref, w_ref, r_ref, o_ref, att_sc):
    @pl.when(pl.program_id(1) == 0)
    def _():
        for hh in range(XATTN_HEADS):
            sl = slice(hh * XATTN_DH, (hh + 1) * XATTN_DH)
            s = _dotg(q_ref[:, sl], k_ref[:, sl], _NT) * (XATTN_DH ** -0.5)
            e = jnp.exp(s - jnp.max(s, axis=1, keepdims=True))
            p = e / jnp.sum(e, axis=1, keepdims=True)
            att_sc[:, sl] = _dot(p.astype(BF16), v_ref[:, sl]).astype(BF16)

    o_ref[...] = r_ref[...] + _dot(att_sc[...], w_ref[...])


def _xattn(q, kv, w_xo, resid):
    tm, tn = XATTN_TM, XATTN_TN
    per_b = SEQ // tm
    return pl.pallas_call(
        _xattn_kernel,
        grid=(T_TOK // tm, D_MODEL // tn),
        in_specs=[
            pl.BlockSpec((tm, D_MODEL), lambda i, j: (i, 0)),
            pl.BlockSpec((MEM_LEN, D_MODEL), lambda i, j: (i // per_b, 0)),
            pl.BlockSpec((MEM_LEN, D_MODEL), lambda i, j: (i // per_b, 1)),
            pl.BlockSpec((D_MODEL, tn), lambda i, j: (0, j)),
            pl.BlockSpec((tm, tn), lambda i, j: (i, j)),
        ],
        out_specs=pl.BlockSpec((tm, tn), lambda i, j: (i, j)),
        out_shape=jax.ShapeDtypeStruct((T_TOK, D_MODEL), F32),
        scratch_shapes=[pltpu.VMEM((tm, D_MODEL), BF16)],
        compiler_params=_params(("parallel", "arbitrary")),
        name="mem_xattn",
    )(q, kv, kv, w_xo, resid)


ROUTER_TM = 512


def _router_kernel(x_ref, g_ref, w_ref, b_ref, idx_ref, gate_ref, xn_ref):
    xn = _rmsnorm(x_ref[...], g_ref[...])
    xn_ref[...] = xn
    w = w_ref[...]
    xh = xn.astype(BF16)
    xl = (xn - xh.astype(F32)).astype(BF16)
    wh = w.astype(BF16)
    wl = (w - wh.astype(F32)).astype(BF16)
    logits = _dot(xh, wh) + (_dot(xh, wl) + _dot(xl, wh)) + b_ref[...]
    lane = lax.broadcasted_iota(jnp.int32, logits.shape, 1).astype(F32)
    kcol = lax.broadcasted_iota(jnp.int32, (logits.shape[0], TOP_K), 1)
    top = jnp.zeros((logits.shape[0], TOP_K), F32)
    sel_all = jnp.zeros((logits.shape[0], TOP_K), F32)
    cur = logits
    for kk in range(TOP_K):
        m = jnp.max(cur, axis=1, keepdims=True)
        sel = jnp.min(jnp.where(cur == m, lane, float(N_EXPERTS)), axis=1, keepdims=True)
        top = jnp.where(kcol == kk, m, top)
        sel_all = jnp.where(kcol == kk, sel, sel_all)
        cur = jnp.where(lane == sel, -jnp.inf, cur)
    e = jnp.exp(top - jnp.max(top, axis=1, keepdims=True))
    gate_ref[...] = e / jnp.sum(e, axis=1, keepdims=True)
    idx_ref[...] = sel_all.astype(jnp.int32)


def _router(h, gain, w_router, b_router):
    tm = ROUTER_TM
    return pl.pallas_call(
        _router_kernel,
        grid=(T_TOK // tm,),
        in_specs=[
            pl.BlockSpec((tm, D_MODEL), lambda i: (i, 0)),
            pl.BlockSpec((1, D_MODEL), lambda i: (0, 0)),
            pl.BlockSpec((D_MODEL, N_EXPERTS), lambda i: (0, 0)),
            pl.BlockSpec((1, N_EXPERTS), lambda i: (0, 0)),
        ],
        out_specs=[
            pl.BlockSpec((tm, TOP_K), lambda i: (i, 0)),
            pl.BlockSpec((tm, TOP_K), lambda i: (i, 0)),
            pl.BlockSpec((tm, D_MODEL), lambda i: (i, 0)),
        ],
        out_shape=[
            jax.ShapeDtypeStruct((T_TOK, TOP_K), jnp.int32),
            jax.ShapeDtypeStruct((T_TOK, TOP_K), F32),
            jax.ShapeDtypeStruct((T_TOK, D_MODEL), F32),
        ],
        compiler_params=_params(("parallel",)),
        name="router",
    )(h, gain, w_router, b_router)


MOE_TMG = 512
MOE_SUB = 256
MOE_NSUB = MOE_TMG // MOE_SUB
MOE_FC = 256
MOE_NC = D_FF // MOE_FC
MOE_GMAX = T_TOK * TOP_K // MOE_TMG + N_EXPERTS


def _swiglu_interleaved(z):
    rows = z.shape[0]
    lane = lax.broadcasted_iota(jnp.int32, (rows, V7X_LANES), 1)
    even = (lane & 1) == 0

    def act(zz):
        gate = jnp.minimum(zz, SWIGLU_LIMIT)
        a = gate * jax.nn.sigmoid(SWIGLU_ALPHA * gate)
        b = jnp.clip(zz, -SWIGLU_LIMIT, SWIGLU_LIMIT) + 1.0
        return jnp.where(even, a, b)

    outs = []
    for m in range(z.shape[1] // (2 * V7X_LANES)):
        ca = act(z[:, 2 * m * V7X_LANES:(2 * m + 1) * V7X_LANES])
        cb = act(z[:, (2 * m + 1) * V7X_LANES:(2 * m + 2) * V7X_LANES])
        pa = ca * pltpu.roll(ca, V7X_LANES - 1, 1)
        pb = cb * pltpu.roll(cb, 1, 1)
        outs.append(jnp.where(even, pa, pb))
    return jnp.concatenate(outs, axis=1).astype(BF16)


def _unit_order_matrix():
    p = np.zeros((MOE_FC, MOE_FC), np.float32)
    half = V7X_LANES // 2
    for blk in range(MOE_FC // V7X_LANES):
        for i in range(half):
            p[blk * V7X_LANES + 2 * i, blk * V7X_LANES + i] = 1.0
            p[blk * V7X_LANES + 2 * i + 1, blk * V7X_LANES + half + i] = 1.0
    return jnp.asarray(p, BF16)


def _moe_kernel(ge_ref, gr_ref, yb_ref, tok_ref,
                xn_hbm, wgu_ref, bgu_ref, wd_ref, bd_ref, perm_ref,
                y_ref,
                xbuf, xbf, wgu_bf, wd_bf, acc, gsem):
    del ge_ref, yb_ref
    g = pl.program_id(0)
    c = pl.program_id(1)
    nrows = gr_ref[g]
    slot = lax.rem(g, 2)

    def sub_copy(row, tok, sl):
        return pltpu.make_async_copy(xn_hbm.at[pl.ds(tok, 1), :],
                                     xbuf.at[sl, pl.ds(row, 1), :], gsem.at[sl])

    def gather_start(gg, sl):
        n = gr_ref[gg]
        for sub in range(MOE_NSUB):
            @pl.when(sub * MOE_SUB < n)
            def _():
                def body(r, carry):
                    row = sub * MOE_SUB + r
                    sub_copy(row, tok_ref[gg * MOE_TMG + row], sl).start()
                    return carry
                lax.fori_loop(0, MOE_SUB, body, 0, unroll=8)

    def gather_wait(sl):
        for sub in range(MOE_NSUB):
            @pl.when(sub * MOE_SUB < nrows)
            def _():
                pltpu.make_async_copy(xn_hbm.at[pl.ds(0, MOE_SUB), :],
                                      xbuf.at[sl, pl.ds(sub * MOE_SUB, MOE_SUB), :], gsem.at[sl]).wait()
        for sub in range(MOE_NSUB):
            @pl.when(sub * MOE_SUB < nrows)
            def _():
                rows = pl.ds(sub * MOE_SUB, MOE_SUB)
                xbf[rows, :] = xbuf[sl, rows, :].astype(BF16)

    @pl.when(c == 0)
    def _():
        @pl.when(g == 0)
        def _():
            gather_start(0, 0)

        @pl.when(g + 1 < MOE_GMAX)
        def _():
            gather_start(g + 1, 1 - slot)

        gather_wait(slot)

    @pl.when((nrows == 0) & (c == MOE_NC - 1))
    def _():
        y_ref[...] = jnp.zeros_like(y_ref)

    @pl.when(nrows > 0)
    def _():
        wgu_bf[...] = wgu_ref[0].astype(BF16)
        wd_bf[...] = wd_ref[0].astype(BF16)
        bias = bgu_ref[0]
        for sub in range(MOE_NSUB):
            rows = pl.ds(sub * MOE_SUB, MOE_SUB)

            @pl.when(sub * MOE_SUB < nrows)
            def _():
                z = _dot(xbf[rows, :], wgu_bf[...]) + bias
                hid = _swiglu_interleaved(z)
                hid = _dot(hid, perm_ref[...]).astype(BF16)
                part = _dot(hid, wd_bf[...])

                @pl.when(c == 0)
                def _():
                    acc[rows, :] = part

                @pl.when(c > 0)
                def _():
                    acc[rows, :] += part

        @pl.when(c == MOE_NC - 1)
        def _():
            for sub in range(MOE_NSUB):
                rows = pl.ds(sub * MOE_SUB, MOE_SUB)

                @pl.when(sub * MOE_SUB < nrows)
                def _():
                    y_ref[rows, :] = acc[rows, :] + bd_ref[0]

                @pl.when(sub * MOE_SUB >= nrows)
                def _():
                    y_ref[rows, :] = jnp.zeros((MOE_SUB, D_MODEL), F32)


def _moe(group_e, group_rows, y_block, slot_tok, xn, w_gate_up, b_gate_up, w_down, b_down):
    fc, nc = MOE_FC, MOE_NC

    def chunk(c, gr, g):
        return jnp.where(gr[g] > 0, c, nc - 1)

    grid_spec = pltpu.PrefetchScalarGridSpec(
        num_scalar_prefetch=4,
        grid=(MOE_GMAX, nc),
        in_specs=[
            pl.BlockSpec(memory_space=pl.ANY),
            pl.BlockSpec((1, D_MODEL, 2 * fc), lambda g, c, ge, gr, yb, tk: (ge[g], 0, chunk(c, gr, g))),
            pl.BlockSpec((1, 1, 2 * fc), lambda g, c, ge, gr, yb, tk: (ge[g], 0, chunk(c, gr, g))),
            pl.BlockSpec((1, fc, D_MODEL), lambda g, c, ge, gr, yb, tk: (ge[g], chunk(c, gr, g), 0)),
            pl.BlockSpec((1, 1, D_MODEL), lambda g, c, ge, gr, yb, tk: (ge[g], 0, 0)),
            pl.BlockSpec((fc, fc), lambda g, c, ge, gr, yb, tk: (0, 0)),
        ],
        out_specs=pl.BlockSpec((MOE_TMG, D_MODEL), lambda g, c, ge, gr, yb, tk: (yb[g], 0)),
        scratch_shapes=[
            pltpu.VMEM((2, MOE_TMG, D_MODEL), F32),
            pltpu.VMEM((MOE_TMG, D_MODEL), BF16),
            pltpu.VMEM((D_MODEL, 2 * fc), BF16),
            pltpu.VMEM((fc, D_MODEL), BF16),
            pltpu.VMEM((MOE_TMG, D_MODEL), F32),
            pltpu.SemaphoreType.DMA((2,)),
        ],
    )
    return pl.pallas_call(
        _moe_kernel,
        grid_spec=grid_spec,
        out_shape=jax.ShapeDtypeStruct((MOE_GMAX * MOE_TMG, D_MODEL), F32),
        compiler_params=_params(("arbitrary", "arbitrary")),
        name="moe_experts",
    )(group_e, group_rows, y_block, slot_tok,
      xn, w_gate_up, b_gate_up.reshape(N_EXPERTS, 1, 2 * D_FF), w_down,
      b_down.reshape(N_EXPERTS, 1, D_MODEL), _unit_order_matrix())


def _route_plan(top_idx):
    e_ids = jnp.arange(N_EXPERTS, dtype=jnp.int32)
    onehot = jnp.any(top_idx[:, :, None] == e_ids[None, None, :], axis=1).astype(jnp.int32)
    counts = jnp.sum(onehot, axis=0)
    rank = jnp.cumsum(onehot, axis=0) - onehot
    ngroups = (counts + MOE_TMG - 1) // MOE_TMG
    gend = jnp.cumsum(ngroups)
    gstart = gend - ngroups
    n_used = gend[-1]
    pos = gstart[top_idx] * MOE_TMG + jnp.take_along_axis(rank, top_idx, axis=1)
    tok = jnp.broadcast_to(jnp.arange(T_TOK, dtype=jnp.int32)[:, None], pos.shape)
    slot_tok = jnp.zeros((MOE_GMAX * MOE_TMG,), jnp.int32).at[pos.reshape(-1)].set(tok.reshape(-1))
    gid = jnp.arange(MOE_GMAX, dtype=jnp.int32)
    last = jnp.maximum(n_used - 1, 0)
    gclip = jnp.minimum(gid, last)
    ge = jnp.minimum(jnp.sum(gend[None, :] <= gclip[:, None], axis=1), N_EXPERTS - 1).astype(jnp.int32)
    rows = jnp.clip(counts[ge] - (gclip - gstart[ge]) * MOE_TMG, 0, MOE_TMG)
    rows = jnp.where(gid < n_used, rows, 0).astype(jnp.int32)
    return ge, rows, gid, slot_tok, pos.reshape(-1).astype(jnp.int32)


COMB_TC = 128


def _combine_kernel(pos_ref, h_ref, gate_ref, g_ref, y_hbm, o_ref, ybuf, sem):
    i = pl.program_id(0)
    n = pl.num_programs(0)
    slot = lax.rem(i, 2)

    def start(ii, sl):
        def body(r, carry):
            for k in range(TOP_K):
                p = pos_ref[(ii * COMB_TC + r) * TOP_K + k]
                pltpu.make_async_copy(y_hbm.at[pl.ds(p, 1), :],
                                      ybuf.at[sl, k, pl.ds(r, 1), :], sem.at[sl]).start()
            return carry
        lax.fori_loop(0, COMB_TC, body, 0, unroll=4)

    @pl.when(i == 0)
    def _():
        start(0, 0)

    @pl.when(i + 1 < n)
    def _():
        start(i + 1, 1 - slot)

    for k in range(TOP_K):
        pltpu.make_async_copy(y_hbm.at[pl.ds(0, COMB_TC), :], ybuf.at[slot, k], sem.at[slot]).wait()

    gate = gate_ref[...]
    out = h_ref[...]
    for k in range(TOP_K):
        out = out + gate[:, k:k + 1] * ybuf[slot, k]
    o_ref[...] = _rmsnorm(out, g_ref[...])


def _combine(pos, h, gates, gain, y):
    tc = COMB_TC
    grid_spec = pltpu.PrefetchScalarGridSpec(
        num_scalar_prefetch=1,
        grid=(T_TOK // tc,),
        in_specs=[
            pl.BlockSpec((tc, D_MODEL), lambda i, p: (i, 0)),
            pl.BlockSpec((tc, TOP_K), lambda i, p: (i, 0)),
            pl.BlockSpec((1, D_MODEL), lambda i, p: (0, 0)),
            pl.BlockSpec(memory_space=pl.ANY),
        ],
        out_specs=pl.BlockSpec((tc, D_MODEL), lambda i, p: (i, 0)),
        scratch_shapes=[
            pltpu.VMEM((2, TOP_K, tc, D_MODEL), F32),
            pltpu.SemaphoreType.DMA((2,)),
        ],
    )
    return pl.pallas_call(
        _combine_kernel,
        grid_spec=grid_spec,
        out_shape=jax.ShapeDtypeStruct((T_TOK, D_MODEL), F32),
        compiler_params=_params(("arbitrary",)),
        name="moe_combine_final_norm",
    )(pos, h, gates, gain, y)


def kernel(x, mem, norm_mix, w_in, w_alpha_up, b_alpha, b_fgate, gla_norm, w_out, norm_xattn, norm_mem,
           w_xq, w_xkv, w_xo, norm_moe, w_router, b_router, w_gate_up, b_gate_up, w_down, b_down, norm_final):
    xf = x.reshape(T_TOK, D_MODEL)
    memf = mem.reshape(BATCH * MEM_LEN, D_MODEL)
    row = lambda v: v.reshape(1, -1)

    w_in0 = w_in[0]
    w_main = jnp.concatenate([w_in0[:, :GLA_COLS], w_in0[:, FOX_COL0:FF_COL0]], axis=1).astype(BF16)
    w_small = jnp.concatenate(
        [w_in0[:, GLA_COLS:FOX_COL0], w_in0[:, FF_COL0:],
         jnp.zeros((D_MODEL, V7X_LANES - GLA_LOWRANK - FOX_HEADS), F32)], axis=1).astype(BF16)

    proj_gla, proj_fox, small = _inproj(xf, row(norm_mix[0]), w_main, w_small)
    w_alpha = jnp.pad(w_alpha_up[0], ((0, V7X_LANES - GLA_LOWRANK), (0, 0)))
    b_fg = jnp.pad(b_fgate[0], (GLA_LOWRANK, V7X_LANES - GLA_LOWRANK - FOX_HEADS))
    o_gla = _gla(proj_gla, small, w_alpha, row(b_alpha[0]), row(gla_norm[0]))
    ccol, crow = _fcum(small, row(b_fg))
    o_fox = _fox(proj_fox, ccol, crow)
    h1 = _outproj(o_gla, o_fox, w_out[0].astype(BF16), xf)

    q = _norm_mm(h1, row(norm_xattn[0]), w_xq[0].astype(BF16), 512, 1024, "xattn_q")
    kv = _norm_mm(memf, row(norm_mem[0]), w_xkv[0].astype(BF16), BATCH * MEM_LEN, 1024, "xattn_kv")
    h2 = _xattn(q, kv, w_xo[0].astype(BF16), h1)

    top_idx, gates, xn = _router(h2, row(norm_moe[0]), w_router[0], row(b_router[0]))
    ge, rows, yblk, slot_tok, pos = _route_plan(top_idx)
    y = _moe(ge, rows, yblk, slot_tok, xn, w_gate_up[0], b_gate_up[0], w_down[0], b_down[0])
    out = _combine(pos, h2, gates, row(norm_final), y)
    return out.reshape(BATCH, SEQ, D_MODEL)
```
